```python
import math
import jax
import jax.numpy as jnp
from jax import lax
import numpy as np

D_MODEL = 2048
BATCH = 4
SEQ = 2048
DEPTH = 1
DEC_BATCH = 128
DEC_SEQ = 1
PAST_LEN = 2048
PAGE_SIZE = 128

HEAD_DIM = 128
ATTN_WIDTH = D_MODEL // 2
N_ATTN_HEADS = ATTN_WIDTH // HEAD_DIM
N_MEM_HEADS = 4
MEM_WIDTH = N_MEM_HEADS * HEAD_DIM
POOL_WIDTH = D_MODEL - ATTN_WIDTH - MEM_WIDTH
POOL_WINDOWS = (2, 4, 8, 16)
N_POOL_GROUPS = len(POOL_WINDOWS)
POOL_GROUP_WIDTH = POOL_WIDTH // N_POOL_GROUPS
POOL_STATE = max(POOL_WINDOWS) - 1
MIX_WIDTH = ATTN_WIDTH + POOL_WIDTH + MEM_WIDTH
IN_WIDTH = 3 * ATTN_WIDTH + POOL_WIDTH + MEM_WIDTH
N_MEM = 256
MOBA_BLOCK = 256
MOBA_TOPK = 3
Q_CHUNK = 16
N_BUCKETS = 32
MAX_DISTANCE = 128
N_GROUPS = 4
EXPERTS_PER_GROUP = 8
N_EXPERTS = N_GROUPS * EXPERTS_PER_GROUP
TOP_K_IN_GROUP = 2
D_EXPERT = D_MODEL // 2
MOE_BLOCK = 64
ALPHA = (2.0 * DEPTH) ** 0.25
BETA = (8.0 * DEPTH) ** -0.25
LN_EPS = 1e-5

kernel_name = 'hybrid_moba_pool_memx_hmoe_step'


def layer_norm(x, g, b):
    xf = x.astype(jnp.float32)
    mu = xf.mean(-1, keepdims=True)
    var = jnp.square(xf - mu).mean(-1, keepdims=True)
    y = (xf - mu) * lax.rsqrt(var + LN_EPS) * g.astype(jnp.float32) + b.astype(jnp.float32)
    return y.astype(x.dtype)


def t5_bucket(dist):
    n = jnp.maximum(dist, 0)
    max_exact = N_BUCKETS // 2
    nf = jnp.maximum(n, max_exact).astype(jnp.float32)
    large = max_exact + (jnp.log(nf / max_exact) / math.log(MAX_DISTANCE / max_exact)
                         * (N_BUCKETS - max_exact)).astype(jnp.int32)
    large = jnp.minimum(large, N_BUCKETS - 1)
    return jnp.where(n < max_exact, n, large)


def moba_attention(q, k, v, q_pos, rel_bias):
    B, Sq, H, Dh = q.shape
    L = k.shape[1]
    nb = -(-L // MOBA_BLOCK)
    pad = nb * MOBA_BLOCK - L
    k = jnp.pad(k, ((0, 0), (0, pad), (0, 0), (0, 0)))
    v = jnp.pad(v, ((0, 0), (0, pad), (0, 0), (0, 0)))
    kb = k.reshape(B, nb, MOBA_BLOCK, H, Dh)
    vb = v.reshape(B, nb, MOBA_BLOCK, H, Dh)
    k_mean = kb.astype(jnp.float32).mean(axis=2)
    n_top = min(MOBA_TOPK, nb)
    chunk = Q_CHUNK if Sq % Q_CHUNK == 0 else Sq
    n_chunks = Sq // chunk
    bias_t = rel_bias.T.astype(jnp.float32)
    b_idx = jnp.arange(B)[:, None, None, None]
    h_idx = jnp.arange(H)[None, :, None, None]
    offs = jnp.arange(MOBA_BLOCK, dtype=jnp.int32)
    scale = HEAD_DIM ** -0.5

    def attend_chunk(args):
        qc, pc = args
        own = pc // MOBA_BLOCK
        gate = jnp.einsum('bchd,bnhd->bhcn', qc.astype(jnp.float32), k_mean)
        past = jnp.arange(nb)[None, :] < own[:, None]
        gate = jnp.where(past, gate, -jnp.inf)
        top_s, top_i = lax.top_k(gate, n_top)
        sel = jnp.concatenate(
            [top_i, jnp.broadcast_to(own[None, None, :, None], (B, H, chunk, 1))], -1)
        sel_ok = jnp.concatenate(
            [jnp.isfinite(top_s), jnp.ones((B, H, chunk, 1), bool)], -1)
        kg = kb[b_idx, sel, :, h_idx, :]
        vg = vb[b_idx, sel, :, h_idx, :]
        k_pos = sel[..., None] * MOBA_BLOCK + offs
        dist = pc[None, None, :, None, None] - k_pos
        valid = sel_ok[..., None] & (dist >= 0)
        bias = bias_t[h_idx[..., None], t5_bucket(dist)]
        logits = jnp.einsum('bchd,bhcnkd->bhcnk', qc, kg,
                            preferred_element_type=jnp.float32) * scale + bias
        logits = jnp.where(valid, logits, -jnp.inf).reshape(B, H, chunk, -1)
        p = jax.nn.softmax(logits, axis=-1).reshape(valid.shape)
        return jnp.einsum('bhcnk,bhcnkd->bchd', p.astype(vg.dtype), vg)

    qs = q.reshape(B, n_chunks, chunk, H, Dh).transpose(1, 0, 2, 3, 4)
    ps = q_pos.reshape(n_chunks, chunk)
    out = lax.map(attend_chunk, (qs, ps))
    return out.transpose(1, 0, 2, 3, 4).reshape(B, Sq, H * Dh)


def multiscale_pool(u_ext, pos0, pool_w, pool_scale):
    B, L, _ = u_ext.shape
    S = L - POOL_STATE
    uf = u_ext.astype(jnp.float32).reshape(B, L, N_POOL_GROUPS, POOL_GROUP_WIDTH)
    csum = jnp.concatenate([jnp.zeros_like(uf[:, :1]), jnp.cumsum(uf, axis=1)], axis=1)
    win = jnp.array(POOL_WINDOWS, jnp.int32)
    hi = POOL_STATE + 1 + jnp.arange(S, dtype=jnp.int32)
    lo = hi[:, None] - win[None, :]
    g_idx = jnp.arange(N_POOL_GROUPS)[None, :]
    window_sum = csum[:, hi[:, None], g_idx] - csum[:, lo, g_idx]
    count = jnp.minimum(pos0 + 1 + jnp.arange(S)[:, None], win[None, :]).astype(jnp.float32)
    diff = window_sum / count[None, :, :, None] - uf[:, POOL_STATE:]
    y = jnp.einsum('bsgc,gcd->bsgd', diff, pool_w.astype(jnp.float32)).reshape(B, S, POOL_WIDTH)
    return (y * pool_scale.astype(jnp.float32)).astype(u_ext.dtype)


def memory_kv(mem, w_mem_kv):
    B, M, _ = mem.shape
    kv = mem @ w_mem_kv
    return (kv[..., :MEM_WIDTH].reshape(B, M, N_MEM_HEADS, HEAD_DIM),
            kv[..., MEM_WIDTH:].reshape(B, M, N_MEM_HEADS, HEAD_DIM))


def memory_attention(qm, mem_k, mem_v):
    B, S, _ = qm.shape
    q = qm.reshape(B, S, N_MEM_HEADS, HEAD_DIM)
    logits = jnp.einsum('bshd,bmhd->bhsm', q, mem_k,
                        preferred_element_type=jnp.float32) * HEAD_DIM ** -0.5
    p = jax.nn.softmax(logits, axis=-1)
    o = jnp.einsum('bhsm,bmhd->bshd', p.astype(mem_v.dtype), mem_v)
    return o.reshape(B, S, MEM_WIDTH)


def routed_experts(xt, e_idx, gate, w_gate, w_up, w_down):
    T, D = xt.shape
    K = e_idx.shape[1]
    A = T * K
    e_flat = e_idx.reshape(A)
    tok_flat = jnp.arange(A, dtype=jnp.int32) // K
    order = jnp.argsort(e_flat)
    e_sorted = e_flat[order]
    tok_sorted = tok_flat[order]
    gate_sorted = gate.reshape(A)[order]
    counts = jnp.bincount(e_flat, length=N_EXPERTS)
    padded = (counts + MOE_BLOCK - 1) // MOE_BLOCK * MOE_BLOCK
    pend = jnp.cumsum(padded)
    pstart = pend - padded
    cstart = jnp.cumsum(counts) - counts
    dest = pstart[e_sorted] + jnp.arange(A) - cstart[e_sorted]
    n_blocks = -(-A // MOE_BLOCK) + N_EXPERTS
    buf_tok = jnp.full((n_blocks * MOE_BLOCK,), T, jnp.int32).at[dest].set(tok_sorted)
    block_start = jnp.arange(n_blocks) * MOE_BLOCK
    block_e = jnp.minimum(jnp.sum(pend[None, :] <= block_start[:, None], axis=1), N_EXPERTS - 1)
    x_pad = jnp.concatenate([xt, jnp.zeros((1, D), xt.dtype)], axis=0)

    def run_block(args):
        rows, e = args
        r = x_pad[rows]
        h = jax.nn.silu(r @ w_gate[e]) * (r @ w_up[e])
        return h @ w_down[e]

    out = lax.map(run_block, (buf_tok.reshape(n_blocks, MOE_BLOCK), block_e))
    y_sorted = out.reshape(-1, D)[dest] * gate_sorted[:, None].astype(out.dtype)
    return jnp.zeros((T, D), xt.dtype).at[tok_sorted].add(y_sorted.astype(xt.dtype))


def hier_moe(x, router_g_w, router_g_b, router_e_w, router_e_b, w_gate, w_up, w_down):
    B, S, D = x.shape
    T = B * S
    xt = x.reshape(T, D)
    g_logits = jnp.dot(xt, router_g_w, preferred_element_type=jnp.float32) + router_g_b.astype(jnp.float32)
    g_prob = jax.nn.softmax(g_logits, axis=-1)
    g_top, g_idx = lax.top_k(g_prob, 1)
    e_logits = (jnp.dot(xt, router_e_w, preferred_element_type=jnp.float32)
                + router_e_b.astype(jnp.float32)).reshape(T, N_GROUPS, EXPERTS_PER_GROUP)
    e_in_group = jnp.take_along_axis(e_logits, g_idx[:, :, None], axis=1)[:, 0]
    e_top, e_loc = lax.top_k(e_in_group, TOP_K_IN_GROUP)
    gate = g_top * jax.nn.softmax(e_top, axis=-1)
    e_idx = g_idx * EXPERTS_PER_GROUP + e_loc
    return routed_experts(xt, e_idx, gate, w_gate, w_up, w_down).reshape(B, S, D)


def decoder_layer(x, pos0, k_past, v_past, pool_prev, mem_k, mem_v, rel_bias,
                  w_in, w_out, pool_w, pool_scale, ln1_g, ln1_b, ln2_g, ln2_b,
                  router_g_w, router_g_b, router_e_w, router_e_b, w_gate, w_up, w_down):
    B, S, _ = x.shape
    h = x @ w_in
    q, k, v, u, qm = jnp.split(
        h, [ATTN_WIDTH, 2 * ATTN_WIDTH, 3 * ATTN_WIDTH, 3 * ATTN_WIDTH + POOL_WIDTH], axis=-1)
    q = q.reshape(B, S, N_ATTN_HEADS, HEAD_DIM)
    k = k.reshape(B, S, N_ATTN_HEADS, HEAD_DIM)
    v = v.reshape(B, S, N_ATTN_HEADS, HEAD_DIM)
    k_all = jnp.concatenate([k_past, k], axis=1)
    v_all = jnp.concatenate([v_past, v], axis=1)
    q_pos = pos0 + jnp.arange(S, dtype=jnp.int32)
    a = moba_attention(q, k_all, v_all, q_pos, rel_bias)
    u_ext = jnp.concatenate([pool_prev, u], axis=1)
    p = multiscale_pool(u_ext, pos0, pool_w, pool_scale)
    m = memory_attention(qm, mem_k, mem_v)
    mix = jnp.concatenate([a, p, m], axis=-1) @ w_out
    x = layer_norm(ALPHA * x + mix, ln1_g, ln1_b)
    ffn = hier_moe(x, router_g_w, router_g_b, router_e_w, router_e_b, w_gate, w_up, w_down)
    x = layer_norm(ALPHA * x + ffn, ln2_g, ln2_b)
    return x, k, v, u_ext[:, -POOL_STATE:]


def setup_inputs(seed: int = 0) -> dict:
    key = jax.random.key(seed)
    ks = jax.random.split(key, 32)
    f32 = jnp.float32

    def nrm(i, shape, scale=1.0):
        return jax.random.normal(ks[i], shape, f32) * scale

    n_pages = PAST_LEN // PAGE_SIZE
    n_used = DEC_BATCH * n_pages
    n_phys = n_used + n_used // 4
    page_table = jax.random.permutation(ks[0], n_phys)[:n_used].reshape(DEC_BATCH, n_pages).astype(jnp.int32)
    return {
        'x_prompt': nrm(1, (BATCH, SEQ, D_MODEL)),
        'x_sample': nrm(2, (DEC_BATCH, DEC_SEQ, D_MODEL)),
        'cache_k': nrm(3, (DEPTH, n_phys, PAGE_SIZE, N_ATTN_HEADS, HEAD_DIM)),
        'cache_v': nrm(4, (DEPTH, n_phys, PAGE_SIZE, N_ATTN_HEADS, HEAD_DIM)),
        'cache_mem_k': nrm(5, (DEPTH, DEC_BATCH, N_MEM, N_MEM_HEADS, HEAD_DIM)),
        'cache_mem_v': nrm(6, (DEPTH, DEC_BATCH, N_MEM, N_MEM_HEADS, HEAD_DIM)),
        'state_pool': nrm(7, (DEPTH, DEC_BATCH, POOL_STATE, POOL_WIDTH)),
        'page_table': page_table,
        'mem_prompt': nrm(8, (BATCH, N_MEM, D_MODEL)),
        'ln_emb_g': 1.0 + nrm(9, (D_MODEL,), 0.02),
        'ln_emb_b': nrm(10, (D_MODEL,), 0.02),
        'rel_bias': nrm(11, (N_BUCKETS, N_ATTN_HEADS), 0.2),
        'w_in': nrm(12, (DEPTH, D_MODEL, IN_WIDTH), D_MODEL ** -0.5),
        'w_out': nrm(13, (DEPTH, MIX_WIDTH, D_MODEL), BETA * MIX_WIDTH ** -0.5),
        'w_mem_kv': nrm(14, (DEPTH, D_MODEL, 2 * MEM_WIDTH), D_MODEL ** -0.5),
        'pool_w': nrm(15, (DEPTH, N_POOL_GROUPS, POOL_GROUP_WIDTH, POOL_GROUP_WIDTH), POOL_GROUP_WIDTH ** -0.5),
        'pool_scale': 1.0 + nrm(16, (DEPTH, POOL_WIDTH), 0.02),
        'ln1_g': 1.0 + nrm(17, (DEPTH, D_MODEL), 0.02),
        'ln1_b': nrm(18, (DEPTH, D_MODEL), 0.02),
        'ln2_g': 1.0 + nrm(19, (DEPTH, D_MODEL), 0.02),
        'ln2_b': nrm(20, (DEPTH, D_MODEL), 0.02),
        'router_g_w': nrm(21, (DEPTH, D_MODEL, N_GROUPS), D_MODEL ** -0.5),
        'router_g_b': nrm(22, (DEPTH, N_GROUPS), 0.01),
        'router_e_w': nrm(23, (DEPTH, D_MODEL, N_EXPERTS), D_MODEL ** -0.5),
        'router_e_b': nrm(24, (DEPTH, N_EXPERTS), 0.01),
        'exp_w_gate': nrm(25, (DEPTH, N_EXPERTS, D_MODEL, D_EXPERT), D_MODEL ** -0.5),
        'exp_w_up': nrm(26, (DEPTH, N_EXPERTS, D_MODEL, D_EXPERT), D_MODEL ** -0.5),
        'exp_w_down': nrm(27, (DEPTH, N_EXPERTS, D_EXPERT, D_MODEL), BETA * D_EXPERT ** -0.5),
    }


def reference(x_prompt, x_sample, cache_k, cache_v, cache_mem_k, cache_mem_v, state_pool, page_table,
              mem_prompt, ln_emb_g, ln_emb_b, rel_bias, w_in, w_out, w_mem_kv, pool_w, pool_scale,
              ln1_g, ln1_b, ln2_g, ln2_b, router_g_w, router_g_b, router_e_w, router_e_b,
              exp_w_gate, exp_w_up, exp_w_down):
    B = x_prompt.shape[0]
    Bd = x_sample.shape[0]
    n_pages = page_table.shape[1]
    past_len = n_pages * cache_k.shape[2]
    xp = layer_norm(x_prompt, ln_emb_g, ln_emb_b)
    xs = layer_norm(x_sample, ln_emb_g, ln_emb_b)
    no_past = jnp.zeros((B, 0, N_ATTN_HEADS, HEAD_DIM), x_prompt.dtype)
    no_pool = jnp.zeros((B, POOL_STATE, POOL_WIDTH), x_prompt.dtype)
    kp, vp, mkp, mvp, pp, ksm, vsm, psm = [], [], [], [], [], [], [], []
    for l in range(DEPTH):
        lw = (w_in[l], w_out[l], pool_w[l], pool_scale[l], ln1_g[l], ln1_b[l], ln2_g[l], ln2_b[l],
              router_g_w[l], router_g_b[l], router_e_w[l], router_e_b[l],
              exp_w_gate[l], exp_w_up[l], exp_w_down[l])
        mem_k_p, mem_v_p = memory_kv(mem_prompt, w_mem_kv[l])
        xp, k_p, v_p, pool_p = decoder_layer(xp, 0, no_past, no_past, no_pool, mem_k_p, mem_v_p,
                                             rel_bias, *lw)
        k_past = cache_k[l][page_table].reshape(Bd, past_len, N_ATTN_HEADS, HEAD_DIM)
        v_past = cache_v[l][page_table].reshape(Bd, past_len, N_ATTN_HEADS, HEAD_DIM)
        xs, k_s, v_s, pool_s = decoder_layer(xs, past_len, k_past, v_past, state_pool[l],
                                             cache_mem_k[l], cache_mem_v[l], rel_bias, *lw)
        kp.append(k_p)
        vp.append(v_p)
        mkp.append(mem_k_p)
        mvp.append(mem_v_p)
        pp.append(pool_p)
        ksm.append(k_s)
        vsm.append(v_s)
        psm.append(pool_s)
    k_prompt = jnp.stack(kp)
    v_prompt = jnp.stack(vp)
    mem_k_prompt = jnp.stack(mkp)
    mem_v_prompt = jnp.stack(mvp)
    pool_prompt = jnp.stack(pp)
    k_sample = jnp.stack(ksm)
    v_sample = jnp.stack(vsm)
    pool_sample = jnp.stack(psm)
    return (xp, xs, k_prompt, v_prompt, mem_k_prompt, mem_v_prompt, pool_prompt, k_sample, v_sample, pool_sample)
```

```python
import functools
import math

import jax
import jax.numpy as jnp
from jax import lax
from jax.experimental import pallas as pl
from jax.experimental.pallas import tpu as pltpu

F32 = jnp.float32
BF16 = jnp.bfloat16
NEG_INF = float("-inf")

D_MODEL = 2048
HEAD_DIM = 128
N_ATTN_HEADS = 8
ATTN_WIDTH = N_ATTN_HEADS * HEAD_DIM
N_MEM_HEADS = 4
MEM_WIDTH = N_MEM_HEADS * HEAD_DIM
POOL_WINDOWS = (2, 4, 8, 16)
POOL_GROUP_WIDTH = 128
POOL_WIDTH = len(POOL_WINDOWS) * POOL_GROUP_WIDTH
POOL_STATE = max(POOL_WINDOWS) - 1
POOL_HALO = POOL_STATE + 1
MOBA_BLOCK = 256
MOBA_TOPK = 3
N_BUCKETS = 32
MAX_DISTANCE = 128
N_GROUPS = 4
EXPERTS_PER_GROUP = 8
N_EXPERTS = N_GROUPS * EXPERTS_PER_GROUP
TOP_K_IN_GROUP = 2
D_EXPERT = D_MODEL // 2
DEPTH = 1
ALPHA = (2.0 * DEPTH) ** 0.25
LN_EPS = 1e-5
ATTN_SCALE = HEAD_DIM ** -0.5

LANES = 128
VMEM_LIMIT_BYTES = 56 * 1024 * 1024

PROJ_TILE_N = 512
EXPERT_TILE_M = 256
ROUTE_LANES = LANES


def _params(semantics, vmem=VMEM_LIMIT_BYTES):
    return pltpu.CompilerParams(dimension_semantics=semantics, vmem_limit_bytes=vmem)


def _layer_norm(x, g, b):
    mu = jnp.mean(x, axis=-1, keepdims=True)
    xc = x - mu
    var = jnp.mean(xc * xc, axis=-1, keepdims=True)
    return xc * lax.rsqrt(var + LN_EPS) * g + b


def _dot_nt(a, b):
    return lax.dot_general(a, b, (((1,), (1,)), ((), ())), preferred_element_type=F32)


def _proj_kernel(*refs, splits, apply_ln):
    if apply_ln:
        x_ref, g_ref, b_ref, w_ref = refs[:4]
        rest = refs[4:]
    else:
        x_ref, w_ref = refs[:2]
        rest = refs[2:]
    out_refs = rest[:len(splits)]
    xn_ref = rest[len(splits)]
    j = pl.program_id(1)

    @pl.when(j == 0)
    def _():
        x = x_ref[...]
        if apply_ln:
            x = _layer_norm(x, g_ref[...], b_ref[...])
        xn_ref[...] = x.astype(BF16)

    acc = jnp.dot(xn_ref[...], w_ref[...], preferred_element_type=F32)
    for o_ref, (start, count) in zip(out_refs, splits):
        @pl.when((j >= start) & (j < start + count))
        def _(o_ref=o_ref):
            o_ref[...] = acc.astype(o_ref.dtype)


def _proj(x, w_bf16, widths, dtypes, ln=None, tile_m=512):
    t, d = x.shape
    n = w_bf16.shape[1]
    tn = PROJ_TILE_N
    tm = min(tile_m, t)
    assert t % tm == 0 and n % tn == 0 and all(w % tn == 0 for w in widths)
    splits, s = [], 0
    for w in widths:
        splits.append((s // tn, w // tn))
        s += w
    assert s == n
    in_specs = [pl.BlockSpec((tm, d), lambda i, j: (i, 0))]
    args = [x]
    if ln is not None:
        in_specs += [pl.BlockSpec((1, d), lambda i, j: (0, 0))] * 2
        args += [ln[0].reshape(1, d), ln[1].reshape(1, d)]
    in_specs.append(pl.BlockSpec((d, tn), lambda i, j: (0, j)))
    args.append(w_bf16)
    out_specs = [
        pl.BlockSpec((tm, tn), lambda i, j, s=s, c=c: (i, jnp.clip(j - s, 0, c - 1)))
        for (s, c) in splits
    ]
    out_shape = [jax.ShapeDtypeStruct((t, w), dt) for w, dt in zip(widths, dtypes)]
    return pl.pallas_call(
        functools.partial(_proj_kernel, splits=tuple(splits), apply_ln=ln is not None),
        grid=(t // tm, n // tn),
        in_specs=in_specs,
        out_specs=out_specs,
        out_shape=out_shape,
        scratch_shapes=[pltpu.VMEM((tm, d), BF16)],
        compiler_params=_params(("arbitrary", "arbitrary")),
        name="proj",
    )(*args)


def _t5_bucket(dist):
    n = jnp.maximum(dist, 0)
    max_exact = N_BUCKETS // 2
    nf = jnp.maximum(n, max_exact).astype(F32)
    large = max_exact + (jnp.log(nf / max_exact) / math.log(MAX_DISTANCE / max_exact)
                         * (N_BUCKETS - max_exact)).astype(jnp.int32)
    large = jnp.minimum(large, N_BUCKETS - 1)
    return jnp.where(n < max_exact, n, large)


def _bias_by_distance(rel_bias, max_dist):
    return rel_bias.astype(F32)[_t5_bucket(jnp.arange(max_dist + 1, dtype=jnp.int32))].T


def _moba_prompt_kernel(far_ref, q_ref, k_ref, v_ref, bias_ref, o_ref, kb_ref, vb_ref, kmean_ref, *, n_blocks):
    h = pl.program_id(1)
    qi = pl.program_id(2)
    blk = MOBA_BLOCK

    @pl.when(qi == 0)
    def _():
        kmean_ref[...] = jnp.zeros_like(kmean_ref)
        for n in range(n_blocks):
            kn = k_ref[n * blk:(n + 1) * blk, :]
            kb_ref[n * blk:(n + 1) * blk, :] = kn.astype(BF16)
            kmean_ref[n:n + 1, :] = jnp.mean(kn, axis=0, keepdims=True)
        vb_ref[...] = v_ref[...].astype(BF16)

    q = q_ref[...]
    qb = q.astype(BF16)
    gate = lax.dot_general(q, kmean_ref[...], (((1,), (1,)), ((), ())),
                           precision=lax.Precision.HIGHEST, preferred_element_type=F32)
    lane = lax.broadcasted_iota(jnp.int32, gate.shape, 1)
    valid = lane < qi
    gv = jnp.where(valid, gate, NEG_INF)
    rank = jnp.zeros(gate.shape, jnp.int32)
    for m in range(n_blocks):
        gm = gv[:, m:m + 1]
        beats = (gm > gv) | ((gm == gv) & (lane > m))
        rank = rank + beats.astype(jnp.int32)
    sel = jnp.where(valid & (rank < MOBA_TOPK), 1.0, 0.0)

    own = pl.multiple_of(qi * blk, blk)
    s = _dot_nt(qb, kb_ref[pl.ds(own, blk), :]) * ATTN_SCALE + bias_ref[0]
    m0 = jnp.max(s, axis=1, keepdims=True)
    p = jnp.exp(s - m0)
    l0 = jnp.sum(p, axis=1, keepdims=True)
    acc0 = jnp.dot(p.astype(BF16), vb_ref[pl.ds(own, blk), :], preferred_element_type=F32)
    far_bias = far_ref[h]

    def body(n, carry):
        m_run, l_run, acc = carry
        start = pl.multiple_of(n * blk, blk)
        s = _dot_nt(qb, kb_ref[pl.ds(start, blk), :]) * ATTN_SCALE
        bias = jnp.where(n == qi - 1, bias_ref[1], far_bias)
        chosen = jnp.max(jnp.where(lane == n, sel, 0.0), axis=1, keepdims=True)
        s = jnp.where(chosen > 0.0, s + bias, NEG_INF)
        m_new = jnp.maximum(m_run, jnp.max(s, axis=1, keepdims=True))
        corr = jnp.exp(m_run - m_new)
        p = jnp.exp(s - m_new)
        l_new = corr * l_run + jnp.sum(p, axis=1, keepdims=True)
        acc = corr * acc + jnp.dot(p.astype(BF16), vb_ref[pl.ds(start, blk), :],
                                   preferred_element_type=F32)
        return m_new, l_new, acc

    _, l_fin, acc = lax.fori_loop(0, qi, body, (m0, l0, acc0))
    o_ref[...] = (acc / l_fin).astype(o_ref.dtype)


def _moba_prompt(q, k, v, rel_bias):
    b, s, _ = q.shape
    blk = MOBA_BLOCK
    assert s % blk == 0
    nb = s // blk
    bd = _bias_by_distance(rel_bias, 2 * blk)
    i = jnp.arange(blk)[:, None]
    j = jnp.arange(blk)[None, :]
    own = jnp.where(j <= i, bd[:, jnp.maximum(i - j, 0)], NEG_INF)
    prev = bd[:, blk + i - j]
    bias_tab = jnp.stack([own, prev], axis=1)
    far = rel_bias.astype(F32)[N_BUCKETS - 1]
    assert blk + 1 > MAX_DISTANCE
    hd = HEAD_DIM
    return pl.pallas_call(
        functools.partial(_moba_prompt_kernel, n_blocks=nb),
        grid_spec=pltpu.PrefetchScalarGridSpec(
            num_scalar_prefetch=0,
            grid=(b, N_ATTN_HEADS, nb),
            in_specs=[
                pl.BlockSpec(memory_space=pltpu.SMEM),
                pl.BlockSpec((None, blk, hd), lambda bi, h, qi: (bi, qi, h)),
                pl.BlockSpec((None, s, hd), lambda bi, h, qi: (bi, 0, h)),
                pl.BlockSpec((None, s, hd), lambda bi, h, qi: (bi, 0, h)),
                pl.BlockSpec((None, 2, blk, blk), lambda bi, h, qi: (h, 0, 0, 0)),
            ],
            out_specs=pl.BlockSpec((None, blk, hd), lambda bi, h, qi: (bi, qi, h)),
            scratch_shapes=[
                pltpu.VMEM((s, hd), BF16),
                pltpu.VMEM((s, hd), BF16),
                pltpu.VMEM((LANES, hd), F32),
            ],
        ),
        out_shape=jax.ShapeDtypeStruct((b, s, N_ATTN_HEADS * hd), BF16),
        compiler_params=_params(("arbitrary", "arbitrary", "arbitrary")),
        name="moba_prompt",
    )(far, q, k, v, bias_tab)


def _moba_sample_kernel(pt_ref, qbd_ref, kpage_ref, vpage_ref, knew_ref, vnew_ref, bias_ref, o_ref,
                        s_ref, vb_ref, *, n_pages, page, past_len):
    del pt_ref
    b = pl.program_id(0)
    j = pl.program_id(1)
    blk = MOBA_BLOCK
    ext = past_len + blk
    n_past = past_len // blk

    @pl.when((b == 0) & (j == 0))
    def _():
        vb_ref[past_len:ext, :] = jnp.zeros((blk, vb_ref.shape[1]), BF16)

    row = pl.multiple_of(j * page, page)
    s_ref[pl.ds(row, page), :] = jnp.dot(kpage_ref[...].astype(BF16), qbd_ref[...],
                                         preferred_element_type=F32)
    vb_ref[pl.ds(row, page), :] = vpage_ref[...].astype(BF16)

    @pl.when(j == n_pages - 1)
    def _():
        nrow = knew_ref.shape[0]
        s_ref[past_len:past_len + nrow, :] = jnp.dot(knew_ref[...].astype(BF16), qbd_ref[...],
                                                     preferred_element_type=F32)
        s_ref[past_len + nrow:ext, :] = jnp.zeros((blk - nrow, LANES), F32)
        vb_ref[past_len:past_len + nrow, :] = vnew_ref[...].astype(BF16)
        gates = [jnp.mean(s_ref[n * blk:(n + 1) * blk, :], axis=0, keepdims=True) for n in range(n_past)]
        m_run = None
        for n in range(n_past + 1):
            logit = s_ref[n * blk:(n + 1) * blk, :] * ATTN_SCALE + bias_ref[n * blk:(n + 1) * blk, :]
            if n < n_past:
                rank = jnp.zeros(gates[n].shape, jnp.int32)
                for m in range(n_past):
                    if m == n:
                        continue
                    beats = (gates[m] > gates[n]) if m > n else (gates[m] >= gates[n])
                    rank = rank + beats.astype(jnp.int32)
                logit = jnp.where(rank < MOBA_TOPK, logit, NEG_INF)
            s_ref[n * blk:(n + 1) * blk, :] = logit
            m_blk = jnp.max(logit, axis=0, keepdims=True)
            m_run = m_blk if m_run is None else jnp.maximum(m_run, m_blk)
        p = jnp.exp(s_ref[...] - m_run)
        p = p / jnp.sum(p, axis=0, keepdims=True)
        full = lax.dot_general(p.astype(BF16), vb_ref[...], (((0,), (0,)), ((), ())),
                               preferred_element_type=F32)
        hrow = lax.broadcasted_iota(jnp.int32, full.shape, 0)
        hcol = lax.broadcasted_iota(jnp.int32, full.shape, 1) // HEAD_DIM
        o_ref[...] = jnp.sum(jnp.where(hrow == hcol, full, 0.0), axis=0, keepdims=True)


def _moba_sample(q, k_new, v_new, cache_k, cache_v, page_table, rel_bias):
    bd_, width = q.shape
    n_phys, page = cache_k.shape[0], cache_k.shape[1]
    n_pages = page_table.shape[1]
    past_len = n_pages * page
    blk = MOBA_BLOCK
    assert past_len % blk == 0 and blk % page == 0
    ext = past_len + blk
    pad_rows = 8
    rows = jnp.arange(width)[:, None] // HEAD_DIM
    cols = jnp.arange(LANES)[None, :]
    qbd = jnp.where(rows == cols, q[:, :, None], 0.0).astype(BF16)
    knew = jnp.pad(k_new[:, None, :], ((0, 0), (0, pad_rows - 1), (0, 0)))
    vnew = jnp.pad(v_new[:, None, :], ((0, 0), (0, pad_rows - 1), (0, 0)))
    bdist = _bias_by_distance(rel_bias, past_len)
    key_pos = jnp.arange(ext)
    bias = jnp.where(key_pos[None, :] <= past_len, bdist[:, jnp.maximum(past_len - key_pos, 0)], NEG_INF)
    bias = jnp.pad(bias.T, ((0, 0), (0, LANES - N_ATTN_HEADS)))
    ck = cache_k.reshape(n_phys, page, width)
    cv = cache_v.reshape(n_phys, page, width)
    out = pl.pallas_call(
        functools.partial(_moba_sample_kernel, n_pages=n_pages, page=page, past_len=past_len),
        grid_spec=pltpu.PrefetchScalarGridSpec(
            num_scalar_prefetch=1,
            grid=(bd_, n_pages),
            in_specs=[
                pl.BlockSpec((None, width, LANES), lambda b, j, pt: (b, 0, 0)),
                pl.BlockSpec((None, page, width), lambda b, j, pt: (pt[b * n_pages + j], 0, 0)),
                pl.BlockSpec((None, page, width), lambda b, j, pt: (pt[b * n_pages + j], 0, 0)),
                pl.BlockSpec((None, pad_rows, width), lambda b, j, pt: (b, 0, 0)),
                pl.BlockSpec((None, pad_rows, width), lambda b, j, pt: (b, 0, 0)),
                pl.BlockSpec((ext, LANES), lambda b, j, pt: (0, 0)),
            ],
            out_specs=pl.BlockSpec((None, 1, width), lambda b, j, pt: (b, 0, 0)),
            scratch_shapes=[
                pltpu.VMEM((ext, LANES), F32),
                pltpu.VMEM((ext, width), BF16),
            ],
        ),
        out_shape=jax.ShapeDtypeStruct((bd_, 1, width), F32),
        compiler_params=_params(("arbitrary", "arbitrary")),
        name="moba_sample",
    )(page_table.reshape(-1), qbd, ck, cv, knew, vnew, bias)
    return out.reshape(bd_, width)


def _pool_prompt_kernel(u_ref, prev_ref, w_ref, scale_ref, o_ref, ext_ref, *, seq, pos0):
    halo = POOL_HALO
    ext_ref[0:halo, :] = prev_ref[...]
    ext_ref[halo:halo + seq, :] = u_ref[...]
    pos = pos0 + 1 + lax.broadcasted_iota(jnp.int32, (seq, 1), 0)
    gw = POOL_GROUP_WIDTH
    for g, win in enumerate(POOL_WINDOWS):
        cols = slice(g * gw, (g + 1) * gw)
        wsum = ext_ref[halo:halo + seq, cols]
        for t in range(1, win):
            wsum = wsum + ext_ref[halo - t:halo - t + seq, cols]
        count = jnp.minimum(pos, win).astype(F32)
        diff = wsum / count - ext_ref[halo:halo + seq, cols]
        y = jnp.dot(diff.astype(BF16), w_ref[g].astype(BF16), preferred_element_type=F32)
        o_ref[:, cols] = (y * scale_ref[:, cols]).astype(o_ref.dtype)


def _pool_prompt(u, prev, pool_w, pool_scale, pos0):
    b, s, w = u.shape
    return pl.pallas_call(
        functools.partial(_pool_prompt_kernel, seq=s, pos0=pos0),
        grid=(b,),
        in_specs=[
            pl.BlockSpec((None, s, w), lambda i: (i, 0, 0)),
            pl.BlockSpec((None, POOL_HALO, w), lambda i: (i, 0, 0)),
            pl.BlockSpec(pool_w.shape, lambda i: (0, 0, 0)),
            pl.BlockSpec((1, w), lambda i: (0, 0)),
        ],
        out_specs=pl.BlockSpec((None, s, w), lambda i: (i, 0, 0)),
        out_shape=jax.ShapeDtypeStruct((b, s, w), BF16),
        scratch_shapes=[pltpu.VMEM((POOL_HALO + s, w), F32)],
        compiler_params=_params(("arbitrary",)),
        name="pool_prompt",
    )(u, prev, pool_w, pool_scale.reshape(1, w))


def _pool_sample_kernel(ext_ref, w_ref, scale_ref, o_ref, *, pos0):
    halo = POOL_HALO
    gw = POOL_GROUP_WIDTH
    for g, win in enumerate(POOL_WINDOWS):
        cols = slice(g * gw, (g + 1) * gw)
        new = ext_ref[:, halo - 1, cols]
        wsum = new
        for t in range(1, win):
            wsum = wsum + ext_ref[:, halo - 1 - t, cols]
        count = float(min(pos0 + 1, win))
        diff = wsum / count - new
        y = jnp.dot(diff.astype(BF16), w_ref[g].astype(BF16), preferred_element_type=F32)
        o_ref[:, cols] = (y * scale_ref[:, cols]).astype(o_ref.dtype)


def _pool_sample(ext, pool_w, pool_scale, pos0):
    b, _, w = ext.shape
    return pl.pallas_call(
        functools.partial(_pool_sample_kernel, pos0=pos0),
        grid=(1,),
        in_specs=[
            pl.BlockSpec(ext.shape, lambda i: (0, 0, 0)),
            pl.BlockSpec(pool_w.shape, lambda i: (0, 0, 0)),
            pl.BlockSpec((1, w), lambda i: (0, 0)),
        ],
        out_specs=pl.BlockSpec((b, w), lambda i: (0, 0)),
        out_shape=jax.ShapeDtypeStruct((b, w), BF16),
        compiler_params=_params(("arbitrary",)),
        name="pool_sample",
    )(ext, pool_w, pool_scale.reshape(1, w))


def _memattn_prompt_kernel(q_ref, k_ref, v_ref, o_ref):
    s = _dot_nt(q_ref[...], k_ref[...].astype(BF16)) * ATTN_SCALE
    m = jnp.max(s, axis=1, keepdims=True)
    p = jnp.exp(s - m)
    p = p / jnp.sum(p, axis=1, keepdims=True)
    o_ref[...] = jnp.dot(p.astype(BF16), v_ref[...].astype(BF16),
                         preferred_element_type=F32).astype(o_ref.dtype)


def _memattn_prompt(qm, mem_k, mem_v, tile_q=512):
    b, s, w = qm.shape
    m = mem_k.shape[1]
    tq = min(tile_q, s)
    hd = HEAD_DIM
    return pl.pallas_call(
        _memattn_prompt_kernel,
        grid=(b, N_MEM_HEADS, s // tq),
        in_specs=[
            pl.BlockSpec((None, tq, hd), lambda bi, h, qi: (bi, qi, h)),
            pl.BlockSpec((None, m, hd), lambda bi, h, qi: (bi, 0, h)),
            pl.BlockSpec((None, m, hd), lambda bi, h, qi: (bi, 0, h)),
        ],
        out_specs=pl.BlockSpec((None, tq, hd), lambda bi, h, qi: (bi, qi, h)),
        out_shape=jax.ShapeDtypeStruct((b, s, w), BF16),
        compiler_params=_params(("arbitrary", "arbitrary", "arbitrary")),
        name="memattn_prompt",
    )(qm, mem_k, mem_v)


def _memattn_sample_kernel(q_ref, k_ref, v_ref, o_ref, *, group):
    hd = HEAD_DIM
    for r in range(group):
        q = q_ref[r:r + 1, :].astype(BF16).astype(F32)
        for h in range(N_MEM_HEADS):
            cols = slice(h * hd, (h + 1) * hd)
            kh = k_ref[r, :, cols].astype(BF16).astype(F32)
            s = jnp.sum(kh * q[:, cols], axis=1, keepdims=True) * ATTN_SCALE
            m = jnp.max(s, axis=0, keepdims=True)
            p = jnp.exp(s - m)
            p = (p / jnp.sum(p, axis=0, keepdims=True)).astype(BF16).astype(F32)
            vh = v_ref[r, :, cols].astype(BF16).astype(F32)
            o_ref[r:r + 1, cols] = jnp.sum(p * vh, axis=0, keepdims=True).astype(o_ref.dtype)


def _memattn_sample(qm, mem_k, mem_v, group=8):
    b, w = qm.shape
    m = mem_k.shape[1]
    assert b % group == 0
    return pl.pallas_call(
        functools.partial(_memattn_sample_kernel, group=group),
        grid=(b // group,),
        in_specs=[
            pl.BlockSpec((group, w), lambda i: (i, 0)),
            pl.BlockSpec((group, m, w), lambda i: (i, 0, 0)),
            pl.BlockSpec((group, m, w), lambda i: (i, 0, 0)),
        ],
        out_specs=pl.BlockSpec((group, w), lambda i: (i, 0)),
        out_shape=jax.ShapeDtypeStruct((b, w), BF16),
        compiler_params=_params(("arbitrary",)),
        name="memattn_sample",
    )(qm, mem_k, mem_v)


def _route(logits):
    lane = lax.broadcasted_iota(jnp.int32, logits.shape, 1)
    big = jnp.int32(LANES)
    gl = jnp.where(lane < N_GROUPS, logits, NEG_INF)
    gmax = jnp.max(gl, axis=1, keepdims=True)
    gsum = jnp.sum(jnp.exp(gl - gmax), axis=1, keepdims=True)
    g_top = 1.0 / gsum
    g_idx = jnp.min(jnp.where(gl == gmax, lane, big), axis=1, keepdims=True)
    lo = N_GROUPS + g_idx * EXPERTS_PER_GROUP
    el = jnp.where((lane >= lo) & (lane < lo + EXPERTS_PER_GROUP), logits, NEG_INF)
    e1 = jnp.max(el, axis=1, keepdims=True)
    i1 = jnp.min(jnp.where(el == e1, lane, big), axis=1, keepdims=True)
    el2 = jnp.where(lane == i1, NEG_INF, el)
    e2 = jnp.max(el2, axis=1, keepdims=True)
    i2 = jnp.min(jnp.where(el2 == e2, lane, big), axis=1, keepdims=True)
    t = jnp.exp(e2 - e1)
    p1 = 1.0 / (1.0 + t)
    p2 = t / (1.0 + t)
    out = jnp.where(lane == 0, (i1 - N_GROUPS).astype(F32), 0.0)
    out = jnp.where(lane == 1, (i2 - N_GROUPS).astype(F32), out)
    out = jnp.where(lane == 2, g_top * p1, out)
    out = jnp.where(lane == 3, g_top * p2, out)
    return out


def _mix_kernel(a_ref, p_ref, m_ref, x_ref, eg_ref, eb_ref, w_ref, g_ref, b_ref, rw_ref, rb_ref,
                x1_ref, route_ref):
    na = a_ref.shape[1]
    npool = p_ref.shape[1]
    mix = jnp.dot(a_ref[...], w_ref[0:na, :], preferred_element_type=F32)
    mix = mix + jnp.dot(p_ref[...], w_ref[na:na + npool, :], preferred_element_type=F32)
    mix = mix + jnp.dot(m_ref[...], w_ref[na + npool:, :], preferred_element_type=F32)
    x = _layer_norm(x_ref[...], eg_ref[...], eb_ref[...])
    x1 = _layer_norm(ALPHA * x + mix, g_ref[...], b_ref[...])
    x1_ref[...] = x1
    logits = jnp.dot(x1, rw_ref[...], precision=lax.Precision.HIGHEST,
                     preferred_element_type=F32) + rb_ref[...]
    route_ref[...] = _route(logits)


def _mix(a, p, m, x_raw, ln_emb, w_out_bf16, ln1, router_w, router_b, tile_m):
    t, d = x_raw.shape
    tm = min(tile_m, t)
    assert t % tm == 0
    row = lambda i: (i, 0)
    fixed = lambda i: (0, 0)
    vec = lambda v: v.reshape(1, -1)
    in_specs = [
        pl.BlockSpec((tm, a.shape[1]), row),
        pl.BlockSpec((tm, p.shape[1]), row),
        pl.BlockSpec((tm, m.shape[1]), row),
        pl.BlockSpec((tm, d), row),
        pl.BlockSpec((1, d), fixed),
        pl.BlockSpec((1, d), fixed),
        pl.BlockSpec(w_out_bf16.shape, fixed),
        pl.BlockSpec((1, d), fixed),
        pl.BlockSpec((1, d), fixed),
        pl.BlockSpec(router_w.shape, fixed),
        pl.BlockSpec((1, ROUTE_LANES), fixed),
    ]
    args = [a, p, m, x_raw, vec(ln_emb[0]), vec(ln_emb[1]), w_out_bf16, vec(ln1[0]), vec(ln1[1]),
            router_w, router_b]
    return pl.pallas_call(
        _mix_kernel,
        grid=(t // tm,),
        in_specs=in_specs,
        out_specs=[
            pl.BlockSpec((tm, d), row),
            pl.BlockSpec((tm, ROUTE_LANES), row),
        ],
        out_shape=[
            jax.ShapeDtypeStruct((t, d), F32),
            jax.ShapeDtypeStruct((t, ROUTE_LANES), F32),
        ],
        compiler_params=_params(("arbitrary",)),
        name="mix",
    )(*args)


def _expert_kernel(be_ref, cnt_ref, slot_ref, gate_ref, wg_ref, wu_ref, wd_ref, x_hbm, y_hbm,
                   xbuf, ybuf, sem_in, sem_out, *, tm):
    del be_ref
    i = pl.program_id(0)
    cnt = cnt_ref[i]
    base = i * tm

    def row_in(r, tok):
        return pltpu.make_async_copy(x_hbm.at[pl.ds(tok, 1), :], xbuf.at[pl.ds(r, 1), :], sem_in)

    def row_out(r, dst):
        return pltpu.make_async_copy(ybuf.at[pl.ds(r, 1), :], y_hbm.at[pl.ds(dst, 1), :], sem_out)

    @pl.when(cnt > 0)
    def _():
        @pl.when(cnt < tm)
        def _():
            xbuf[...] = jnp.zeros_like(xbuf)

        def start_in(r, c):
            row_in(r, lax.shift_right_logical(slot_ref[base + r], 1)).start()
            return c

        def wait_in(r, c):
            row_in(r, 0).wait()
            return c

        lax.fori_loop(0, cnt, start_in, 0)
        lax.fori_loop(0, cnt, wait_in, 0)
        x = xbuf[...].astype(BF16)
        hg = jnp.dot(x, wg_ref[...], preferred_element_type=F32)
        hu = jnp.dot(x, wu_ref[...], preferred_element_type=F32)
        hidden = (hg * jax.nn.sigmoid(hg)) * hu
        y = jnp.dot(hidden.astype(BF16), wd_ref[...], preferred_element_type=F32)
        ybuf[...] = y * gate_ref[...]

        def start_out(r, c):
            row_out(r, slot_ref[base + r]).start()
            return c

        def wait_out(r, c):
            row_out(r, 0).wait()
            return c

        lax.fori_loop(0, cnt, start_out, 0)
        lax.fori_loop(0, cnt, wait_out, 0)


def _experts(x1, e_idx, gate, wg, wu, wd):
    t, d = x1.shape
    k = e_idx.shape[1]
    a = t * k
    tm = EXPERT_TILE_M
    n_blocks = -(-a // tm) + N_EXPERTS
    e_flat = e_idx.reshape(a)
    onehot = (e_flat[:, None] == jnp.arange(N_EXPERTS)[None, :]).astype(jnp.int32)
    csum = jnp.cumsum(onehot, axis=0)
    rank = jnp.sum(onehot * csum, axis=1) - 1
    counts = csum[-1]
    padded = (counts + tm - 1) // tm * tm
    pend = jnp.cumsum(padded)
    pstart = pend - padded
    dest = pstart[e_flat] + rank
    slot_a = jnp.zeros((n_blocks * tm,), jnp.int32).at[dest].set(jnp.arange(a, dtype=jnp.int32))
    gate_slot = jnp.zeros((n_blocks * tm,), F32).at[dest].set(gate.reshape(a)).reshape(-1, 1)
    block_start = jnp.arange(n_blocks, dtype=jnp.int32) * tm
    block_e = jnp.minimum(jnp.sum(pend[None, :] <= block_start[:, None], axis=1), N_EXPERTS - 1)
    block_cnt = jnp.clip(counts[block_e] - (block_start - pstart[block_e]), 0, tm)
    last_e = block_e[jnp.maximum(pend[-1] // tm - 1, 0)]
    block_e = jnp.where(block_start < pend[-1], block_e, last_e).astype(jnp.int32)
    block_cnt = jnp.where(block_start < pend[-1], block_cnt, 0).astype(jnp.int32)
    f = wg.shape[2]
    return pl.pallas_call(
        functools.partial(_expert_kernel, tm=tm),
        grid_spec=pltpu.PrefetchScalarGridSpec(
            num_scalar_prefetch=3,
            grid=(n_blocks,),
            in_specs=[
                pl.BlockSpec((tm, 1), lambda i, be, cnt, sl: (i, 0)),
                pl.BlockSpec((None, d, f), lambda i, be, cnt, sl: (be[i], 0, 0)),
                pl.BlockSpec((None, d, f), lambda i, be, cnt, sl: (be[i], 0, 0)),
                pl.BlockSpec((None, f, d), lambda i, be, cnt, sl: (be[i], 0, 0)),
                pl.BlockSpec(memory_space=pl.ANY),
            ],
            out_specs=pl.BlockSpec(memory_space=pl.ANY),
            scratch_shapes=[
                pltpu.VMEM((tm, d), F32),
                pltpu.VMEM((tm, d), F32),
                pltpu.SemaphoreType.DMA,
                pltpu.SemaphoreType.DMA,
            ],
        ),
        out_shape=jax.ShapeDtypeStruct((a, d), F32),
        compiler_params=_params(("arbitrary",)),
        name="experts",
    )(block_e, block_cnt, slot_a, gate_slot, wg, wu, wd, x1)


def _final_kernel(x1_ref, y_ref, g_ref, b_ref, op_ref, os_ref, *, n_prompt_tiles):
    d = x1_ref.shape[1]
    i = pl.program_id(0)
    ffn = y_ref[:, 0:d]
    for kk in range(1, y_ref.shape[1] // d):
        ffn = ffn + y_ref[:, kk * d:(kk + 1) * d]
    out = _layer_norm(ALPHA * x1_ref[...] + ffn, g_ref[...], b_ref[...])

    @pl.when(i < n_prompt_tiles)
    def _():
        op_ref[...] = out

    @pl.when(i >= n_prompt_tiles)
    def _():
        os_ref[...] = out


def _final(x1, y_pairs, ln2, n_prompt, tile_m=128):
    t, d = x1.shape
    tm = tile_m
    n_sample = t - n_prompt
    assert n_prompt % tm == 0 and n_sample == tm
    npt = n_prompt // tm
    return pl.pallas_call(
        functools.partial(_final_kernel, n_prompt_tiles=npt),
        grid=(t // tm,),
        in_specs=[
            pl.BlockSpec((tm, d), lambda i: (i, 0)),
            pl.BlockSpec((tm, y_pairs.shape[1]), lambda i: (i, 0)),
            pl.BlockSpec((1, d), lambda i: (0, 0)),
            pl.BlockSpec((1, d), lambda i: (0, 0)),
        ],
        out_specs=[
            pl.BlockSpec((tm, d), lambda i: (jnp.minimum(i, npt - 1), 0)),
            pl.BlockSpec((tm, d), lambda i: (0, 0)),
        ],
        out_shape=[
            jax.ShapeDtypeStruct((n_prompt, d), F32),
            jax.ShapeDtypeStruct((n_sample, d), F32),
        ],
        compiler_params=_params(("arbitrary",)),
        name="final",
    )(x1, y_pairs, ln2[0].reshape(1, d), ln2[1].reshape(1, d))


def kernel(x_prompt, x_sample, cache_k, cache_v, cache_mem_k, cache_mem_v, state_pool, page_table, mem_prompt, ln_emb_g, ln_emb_b, rel_bias, w_in, w_out, w_mem_kv, pool_w, pool_scale, ln1_g, ln1_b, ln2_g, ln2_b, router_g_w, router_g_b, router_e_w, router_e_b, exp_w_gate, exp_w_up, exp_w_down):
    assert w_in.shape[0] == DEPTH == 1
    b, s, d = x_prompt.shape
    bd_, sd, _ = x_sample.shape
    assert sd == 1
    tp = b * s
    n_mem = mem_prompt.shape[1]
    past_len = page_table.shape[1] * cache_k.shape[2]
    l = 0
    ln_emb = (ln_emb_g, ln_emb_b)
    w_in_b = w_in[l].astype(BF16)
    w_out_b = w_out[l].astype(BF16)
    w_mem_b = w_mem_kv[l].astype(BF16)
    widths = (ATTN_WIDTH, ATTN_WIDTH, ATTN_WIDTH, POOL_WIDTH, MEM_WIDTH)
    dtypes = (F32, F32, F32, F32, BF16)
    router_w = jnp.pad(jnp.concatenate([router_g_w[l], router_e_w[l]], axis=1),
                       ((0, 0), (0, ROUTE_LANES - N_GROUPS - N_EXPERTS)))
    router_b = jnp.pad(jnp.concatenate([router_g_b[l], router_e_b[l]]),
                       (0, ROUTE_LANES - N_GROUPS - N_EXPERTS)).reshape(1, ROUTE_LANES)

    xp = x_prompt.reshape(tp, d)
    q_p, k_p, v_p, u_p, qm_p = _proj(xp, w_in_b, widths, dtypes, ln=ln_emb)
    mk_p, mv_p = _proj(mem_prompt.reshape(b * n_mem, d), w_mem_b, (MEM_WIDTH, MEM_WIDTH), (F32, F32))
    a_p = _moba_prompt(q_p.reshape(b, s, -1), k_p.reshape(b, s, -1), v_p.reshape(b, s, -1), rel_bias)
    u_p3 = u_p.reshape(b, s, POOL_WIDTH)
    pool_p = _pool_prompt(u_p3, jnp.zeros((b, POOL_HALO, POOL_WIDTH), F32), pool_w[l], pool_scale[l], 0)
    m_p = _memattn_prompt(qm_p.reshape(b, s, -1), mk_p.reshape(b, n_mem, -1), mv_p.reshape(b, n_mem, -1))

    xs = x_sample.reshape(bd_, d)
    q_s, k_s, v_s, u_s, qm_s = _proj(xs, w_in_b, widths, dtypes, ln=ln_emb)
    a_s = _moba_sample(q_s, k_s, v_s, cache_k[l], cache_v[l], page_table, rel_bias).astype(BF16)
    pool_state_s = jnp.concatenate([state_pool[l][:, 1:], u_s[:, None, :]], axis=1)
    ext_s = jnp.concatenate([state_pool[l][:, :1], pool_state_s], axis=1)
    pool_s = _pool_sample(ext_s, pool_w[l], pool_scale[l], past_len)
    m_s = _memattn_sample(qm_s, cache_mem_k[l].reshape(bd_, n_mem, -1), cache_mem_v[l].reshape(bd_, n_mem, -1))

    ln1 = (ln1_g[l], ln1_b[l])
    x1_p, route_p = _mix(a_p.reshape(tp, -1), pool_p.reshape(tp, -1), m_p.reshape(tp, -1), xp, ln_emb,
                         w_out_b, ln1, router_w, router_b, 256)
    x1_s, route_s = _mix(a_s, pool_s, m_s, xs, ln_emb, w_out_b, ln1, router_w, router_b, bd_)
    x1 = jnp.concatenate([x1_p, x1_s], axis=0)
    route = jnp.concatenate([route_p, route_s], axis=0)
    e_idx = route[:, 0:TOP_K_IN_GROUP].astype(jnp.int32)
    gate = route[:, TOP_K_IN_GROUP:2 * TOP_K_IN_GROUP]
    y = _experts(x1, e_idx, gate, exp_w_gate[l].astype(BF16), exp_w_up[l].astype(BF16),
                 exp_w_down[l].astype(BF16))
    y_p, y_s = _final(x1, y.reshape(tp + bd_, TOP_K_IN_GROUP * d), (ln2_g[l], ln2_b[l]), tp)

    hshape = (N_ATTN_HEADS, HEAD_DIM)
    mshape = (N_MEM_HEADS, HEAD_DIM)
    return (
        y_p.reshape(b, s, d),
        y_s.reshape(bd_, 1, d),
        k_p.reshape(1, b, s, *hshape),
        v_p.reshape(1, b, s, *hshape),
        mk_p.reshape(1, b, n_mem, *mshape),
        mv_p.reshape(1, b, n_mem, *mshape),
        u_p3[:, s - POOL_STATE:][None],
        k_s.reshape(1, bd_, 1, *hshape),
        v_s.reshape(1, bd_, 1, *hshape),
        pool_state_s[None],
    )
```
